```python
import math
import jax, jax.numpy as jnp
from jax import lax
import numpy as np

D_MODEL = 1024
BATCH = 2
SEQ = 8192
DEPTH = 1

CHUNK = 64
LEFT_CHUNKS = 8
BAND = LEFT_CHUNKS + 1
ATTN_HEADS = 8
HEAD_DIM = 64
ATTN_WIDTH = ATTN_HEADS * HEAD_DIM
REL_MAX = 256
REL_SIZE = (CHUNK - 1) + REL_MAX + 1
SSM_WIDTH = D_MODEL // 2
SSM_GROUP = 16
SSM_GROUPS = SSM_WIDTH // SSM_GROUP
SSM_STATE = 64
DT_MIN = 0.001
DT_MAX = 0.1
N_BRANCHES = 2
IN_WIDTH = 3 * ATTN_WIDTH + SSM_WIDTH + N_BRANCHES * D_MODEL
N_EXPERTS = 32
TOP_K = 4
D_EXPERT = D_MODEL
SWIGLU_LIMIT = 7.0
SWIGLU_ALPHA = 1.702
EXPERT_BLOCK = 256
LN_EPS = 1e-5
DEEPNORM_ALPHA = (2 * DEPTH) ** 0.25
DEEPNORM_BETA = (8 * DEPTH) ** -0.25
NEG_INF = -1e30

kernel_name = "streaming_hybrid_relattn_s5_moe"


def layer_norm(x, g, b):
    xf = x.astype(jnp.float32)
    mu = jnp.mean(xf, axis=-1, keepdims=True)
    var = jnp.mean(jnp.square(xf - mu), axis=-1, keepdims=True)
    y = (xf - mu) * lax.rsqrt(var + LN_EPS) * g.astype(jnp.float32) + b.astype(jnp.float32)
    return y.astype(x.dtype)


def chunked_rel_attention(q, k, v, rel_bias):
    bsz, s, _ = q.shape
    nc = s // CHUNK
    shp = (bsz, nc, CHUNK, ATTN_HEADS, HEAD_DIM)
    q = q.reshape(shp) * (HEAD_DIM ** -0.5)
    k = k.reshape(shp)
    v = v.reshape(shp)
    pad = ((0, 0), (LEFT_CHUNKS, 0), (0, 0), (0, 0), (0, 0))
    k_pad = jnp.pad(k, pad)
    v_pad = jnp.pad(v, pad)
    k_band = jnp.concatenate([k_pad[:, j:j + nc] for j in range(BAND)], axis=2)
    v_band = jnp.concatenate([v_pad[:, j:j + nc] for j in range(BAND)], axis=2)
    qc = jnp.arange(CHUNK)[:, None]
    kk = jnp.arange(BAND * CHUNK)[None, :]
    dist = qc - (kk - LEFT_CHUNKS * CHUNK)
    idx = jnp.clip(dist, -(CHUNK - 1), REL_MAX) + (CHUNK - 1)
    bias = rel_bias.astype(jnp.float32)[:, idx]
    key_chunk = jnp.arange(nc)[:, None] - LEFT_CHUNKS + jnp.arange(BAND * CHUNK)[None, :] // CHUNK
    valid = key_chunk >= 0
    scores = jnp.einsum('bnqhd,bnkhd->bnhqk', q, k_band,
                        preferred_element_type=jnp.float32) + bias[None, None]
    scores = jnp.where(valid[None, :, None, None, :], scores, NEG_INF)
    p = jax.nn.softmax(scores, axis=-1).astype(v.dtype)
    o = jnp.einsum('bnhqk,bnkhd->bnqhd', p, v_band)
    return o.reshape(bsz, s, ATTN_WIDTH)


def s5_ssm(u, lam_re, lam_im, log_dt, b_re, b_im, c_re, c_im, d_skip):
    bsz, s, _ = u.shape
    f32 = jnp.float32
    uf = u.astype(f32).reshape(bsz, s, SSM_GROUPS, SSM_GROUP)
    lam = lax.complex(lam_re.astype(f32), lam_im.astype(f32))
    dt = jnp.exp(log_dt.astype(f32))[:, None]
    lam_bar = jnp.exp(lam * dt)
    b_mat = lax.complex(b_re.astype(f32), b_im.astype(f32))
    b_bar = ((lam_bar - 1.0) / lam)[..., None] * b_mat
    c_mat = lax.complex(c_re.astype(f32), c_im.astype(f32))
    bu = jnp.einsum('bsgc,gpc->bsgp', uf.astype(jnp.complex64), b_bar)
    a = jnp.broadcast_to(lam_bar, bu.shape)

    def combine(e1, e2):
        a1, x1 = e1
        a2, x2 = e2
        return a1 * a2, a2 * x1 + x2

    _, states = lax.associative_scan(combine, (a, bu), axis=1)
    y = jnp.einsum('bsgp,gcp->bsgc', states, c_mat).real
    y = y + d_skip.astype(f32).reshape(SSM_GROUPS, SSM_GROUP) * uf
    return y.reshape(bsz, s, SSM_WIDTH).astype(u.dtype)


def moe_ffn(h, router_w, router_b, w_gate_up, b_gate_up, w_down, b_down):
    bsz, s, d = h.shape
    t = bsz * s
    hf = h.reshape(t, d)
    logits = (hf @ router_w + router_b).astype(jnp.float32)
    top_vals, top_idx = lax.top_k(logits, TOP_K)
    gates = jax.nn.softmax(top_vals, axis=-1)
    n_assign = t * TOP_K
    flat_e = top_idx.reshape(-1)
    flat_tok = jnp.repeat(jnp.arange(t, dtype=jnp.int32), TOP_K)
    flat_g = gates.reshape(-1)
    order = jnp.argsort(flat_e)
    sorted_e = flat_e[order]
    counts = jnp.bincount(flat_e, length=N_EXPERTS)
    starts = jnp.cumsum(counts) - counts
    padded = ((counts + EXPERT_BLOCK - 1) // EXPERT_BLOCK) * EXPERT_BLOCK
    pends = jnp.cumsum(padded)
    pstarts = pends - padded
    rank = jnp.arange(n_assign) - starts[sorted_e]
    dest = pstarts[sorted_e] + rank
    n_rows = ((n_assign + N_EXPERTS * EXPERT_BLOCK + EXPERT_BLOCK - 1) // EXPERT_BLOCK) * EXPERT_BLOCK
    n_blocks = n_rows // EXPERT_BLOCK
    tok_pad = jnp.full((n_rows,), t, jnp.int32).at[dest].set(flat_tok[order])
    gate_pad = jnp.zeros((n_rows,), jnp.float32).at[dest].set(flat_g[order])
    blk_start = jnp.arange(n_blocks) * EXPERT_BLOCK
    blk_expert = jnp.minimum(jnp.searchsorted(pends, blk_start, side='right'), N_EXPERTS - 1)
    x_tok = jnp.concatenate([hf, jnp.zeros((1, d), hf.dtype)], axis=0)
    xb = x_tok[tok_pad].reshape(n_blocks, EXPERT_BLOCK, d)

    def expert_block(args):
        xblk, e = args
        gu = xblk @ w_gate_up[e] + b_gate_up[e]
        x_glu, x_lin = jnp.split(gu, 2, axis=-1)
        x_glu = jnp.minimum(x_glu, SWIGLU_LIMIT)
        x_lin = jnp.clip(x_lin, -SWIGLU_LIMIT, SWIGLU_LIMIT)
        act = x_glu * jax.nn.sigmoid(SWIGLU_ALPHA * x_glu) * (x_lin + 1.0)
        return act @ w_down[e] + b_down[e]

    yb = lax.map(expert_block, (xb, blk_expert)).reshape(n_rows, d)
    y = jnp.zeros((t + 1, d), yb.dtype).at[tok_pad].add(yb * gate_pad[:, None].astype(yb.dtype))
    return y[:t].reshape(bsz, s, d)


def setup_inputs(seed: int = 0) -> dict:
    key = jax.random.key(seed)
    ks = jax.random.split(key, 32)
    f32 = jnp.float32

    def nrm(k, shape, scale):
        return jax.random.normal(k, shape, f32) * scale

    L, D, E = DEPTH, D_MODEL, N_EXPERTS
    G, P, Hg = SSM_GROUPS, SSM_STATE, SSM_GROUP
    lam_im0 = jnp.pi * jnp.arange(P, dtype=f32)
    return {
        'x': nrm(ks[0], (BATCH, SEQ, D), 1.0),
        'ln_in_g': 1.0 + nrm(ks[1], (D,), 0.02),
        'ln_in_b': nrm(ks[2], (D,), 0.02),
        'w_in': nrm(ks[3], (L, D, IN_WIDTH), D ** -0.5),
        'b_gate': nrm(ks[4], (L, N_BRANCHES * D), 0.02),
        'rel_bias': nrm(ks[5], (L, ATTN_HEADS, REL_SIZE), 0.2),
        'lam_re': -0.5 + nrm(ks[6], (L, G, P), 0.01),
        'lam_im': lam_im0 + nrm(ks[7], (L, G, P), 0.01),
        'log_dt': jax.random.uniform(ks[8], (L, G), f32, math.log(DT_MIN), math.log(DT_MAX)),
        'ssm_b_re': nrm(ks[9], (L, G, P, Hg), (2 * Hg) ** -0.5),
        'ssm_b_im': nrm(ks[10], (L, G, P, Hg), (2 * Hg) ** -0.5),
        'ssm_c_re': nrm(ks[11], (L, G, Hg, P), (2 * P) ** -0.5),
        'ssm_c_im': nrm(ks[12], (L, G, Hg, P), (2 * P) ** -0.5),
        'ssm_d': nrm(ks[13], (L, SSM_WIDTH), 1.0),
        'w_glu': nrm(ks[14], (L, SSM_WIDTH, 2 * SSM_WIDTH), SSM_WIDTH ** -0.5),
        'w_up_attn': nrm(ks[15], (L, ATTN_WIDTH, D), ATTN_WIDTH ** -0.5),
        'w_up_ssm': nrm(ks[16], (L, SSM_WIDTH, D), SSM_WIDTH ** -0.5),
        'w_o': nrm(ks[17], (L, D, D), DEEPNORM_BETA * D ** -0.5),
        'ln1_g': 1.0 + nrm(ks[18], (L, D), 0.02),
        'ln1_b': nrm(ks[19], (L, D), 0.02),
        'router_w': nrm(ks[20], (L, D, E), D ** -0.5),
        'router_b': nrm(ks[21], (L, E), 0.01),
        'w_gate_up': nrm(ks[22], (L, E, D, 2 * D_EXPERT), D ** -0.5),
        'b_gate_up': nrm(ks[23], (L, E, 2 * D_EXPERT), 0.01),
        'w_down': nrm(ks[24], (L, E, D_EXPERT, D), DEEPNORM_BETA * D_EXPERT ** -0.5),
        'b_down': nrm(ks[25], (L, E, D), 0.01),
        'ln2_g': 1.0 + nrm(ks[26], (L, D), 0.02),
        'ln2_b': nrm(ks[27], (L, D), 0.02),
    }


def reference(x, ln_in_g, ln_in_b, w_in, b_gate, rel_bias, lam_re, lam_im, log_dt,
              ssm_b_re, ssm_b_im, ssm_c_re, ssm_c_im, ssm_d, w_glu, w_up_attn, w_up_ssm,
              w_o, ln1_g, ln1_b, router_w, router_b, w_gate_up, b_gate_up, w_down, b_down,
              ln2_g, ln2_b):
    splits = [ATTN_WIDTH, 2 * ATTN_WIDTH, 3 * ATTN_WIDTH, 3 * ATTN_WIDTH + SSM_WIDTH]
    h = layer_norm(x, ln_in_g, ln_in_b)
    for l in range(DEPTH):
        proj = h @ w_in[l]
        q, k, v, u, gate_logits = jnp.split(proj, splits, axis=-1)
        gates = jax.nn.sigmoid(gate_logits + b_gate[l])
        g_attn, g_ssm = jnp.split(gates, 2, axis=-1)
        y_attn = chunked_rel_attention(q, k, v, rel_bias[l]) @ w_up_attn[l]
        y_s = s5_ssm(u, lam_re[l], lam_im[l], log_dt[l], ssm_b_re[l], ssm_b_im[l],
                     ssm_c_re[l], ssm_c_im[l], ssm_d[l])
        glu_a, glu_b = jnp.split(jax.nn.gelu(y_s) @ w_glu[l], 2, axis=-1)
        y_ssm = (glu_a * jax.nn.sigmoid(glu_b)) @ w_up_ssm[l]
        mix = (g_attn * y_attn + g_ssm * y_ssm) @ w_o[l]
        h = layer_norm(DEEPNORM_ALPHA * h + mix, ln1_g[l], ln1_b[l])
        ffn = moe_ffn(h, router_w[l], router_b[l], w_gate_up[l], b_gate_up[l],
                      w_down[l], b_down[l])
        h = layer_norm(DEEPNORM_ALPHA * h + ffn, ln2_g[l], ln2_b[l])
    return h
```

```python
import functools
import math

import jax
import jax.numpy as jnp
from jax import lax
from jax.experimental import pallas as pl
from jax.experimental.pallas import tpu as pltpu

F32 = jnp.float32
BF16 = jnp.bfloat16
I32 = jnp.int32

D_MODEL = 1024
BATCH = 2
SEQ = 8192
TOKENS = BATCH * SEQ
CHUNK = 64
LEFT_CHUNKS = 8
ATTN_HEADS = 8
HEAD_DIM = 64
ATTN_WIDTH = ATTN_HEADS * HEAD_DIM
REL_MAX = 256
SSM_WIDTH = 512
SSM_GROUP = 16
SSM_GROUPS = 32
SSM_STATE = 64
N_EXPERTS = 32
TOP_K = 4
D_EXPERT = 1024
SWIGLU_LIMIT = 7.0
SWIGLU_ALPHA = 1.702
EXPERT_BLOCK = 256
LN_EPS = 1e-5
DEPTH = 1
DEEPNORM_ALPHA = (2 * DEPTH) ** 0.25
NEG_INF = -1e30

LANES = 128
SUBLANES = 8
ROW_TILES = D_MODEL // LANES

N_ASSIGN = TOKENS * TOP_K
N_ROWS = ((N_ASSIGN + N_EXPERTS * EXPERT_BLOCK + EXPERT_BLOCK - 1) // EXPERT_BLOCK) * EXPERT_BLOCK
N_BLOCKS = N_ROWS // EXPERT_BLOCK

TM = 512
AQ = 256
AK = AQ + LEFT_CHUNKS * CHUNK
SSM_L = 16
SSM_NCH = TOKENS // SSM_L
SSM_CPB = SEQ // SSM_L
SSM_SEG = 64
SSM_NSEG = SSM_NCH // SSM_SEG
SSM_PAIRS = SSM_GROUPS // 2

VMEM_LIMIT = 56 * 1024 * 1024


def _layer_norm(x, g, b):
    mu = jnp.mean(x, axis=-1, keepdims=True)
    xc = x - mu
    var = jnp.mean(xc * xc, axis=-1, keepdims=True)
    return xc * lax.rsqrt(var + LN_EPS) * g + b


def _sigmoid(x):
    return 1.0 / (1.0 + jnp.exp(-x))


def _proj_kernel(x_ref, g_ref, b_ref, w_ref, q_ref, k_ref, v_ref, u_ref):
    h = _layer_norm(x_ref[...], g_ref[...], b_ref[...]).astype(BF16)
    p = jnp.dot(h, w_ref[...], preferred_element_type=F32)
    q_ref[...] = (p[:, 0:ATTN_WIDTH] * (HEAD_DIM ** -0.5)).astype(BF16)
    k_ref[...] = p[:, ATTN_WIDTH:2 * ATTN_WIDTH].astype(BF16)
    v_ref[...] = p[:, 2 * ATTN_WIDTH:3 * ATTN_WIDTH].astype(BF16)
    u_ref[...] = p[:, 3 * ATTN_WIDTH:3 * ATTN_WIDTH + SSM_WIDTH].astype(BF16)


def _proj(x2, ln_g, ln_b, w_qkvu):
    n_in = w_qkvu.shape[1]
    row = lambda i: (i, 0)
    const = lambda i: (0, 0)
    out = jax.ShapeDtypeStruct((TOKENS, ATTN_WIDTH), BF16)
    return pl.pallas_call(
        _proj_kernel,
        grid=(TOKENS // TM,),
        in_specs=[
            pl.BlockSpec((TM, D_MODEL), row),
            pl.BlockSpec((1, D_MODEL), const),
            pl.BlockSpec((1, D_MODEL), const),
            pl.BlockSpec((D_MODEL, n_in), const),
        ],
        out_specs=[pl.BlockSpec((TM, ATTN_WIDTH), row)] * 4,
        out_shape=[out] * 4,
        compiler_params=pltpu.CompilerParams(
            dimension_semantics=("arbitrary",), vmem_limit_bytes=VMEM_LIMIT),
        name="proj",
    )(x2, ln_g, ln_b, w_qkvu)


def _attn_kernel(q_ref, k0_ref, k1_ref, k2_ref, v0_ref, v1_ref, v2_ref, bias_ref, o_ref):
    i = pl.program_id(1)
    first_valid = jnp.maximum(LEFT_CHUNKS * CHUNK - AQ * i, 0)
    col = lax.broadcasted_iota(I32, (AQ, AK), 1)
    valid = col >= first_valid
    kk = jnp.concatenate([k0_ref[...], k1_ref[...], k2_ref[...]], axis=0)
    vv = jnp.concatenate([v0_ref[...], v1_ref[...], v2_ref[...]], axis=0)
    for h in range(ATTN_HEADS):
        sl = slice(h * HEAD_DIM, (h + 1) * HEAD_DIM)
        s = lax.dot_general(q_ref[:, sl], kk[:, sl], (((1,), (1,)), ((), ())),
                            preferred_element_type=F32)
        s = jnp.where(valid, s + bias_ref[h], NEG_INF)
        m = jnp.max(s, axis=-1, keepdims=True)
        p = jnp.exp(s - m)
        l = jnp.sum(p, axis=-1, keepdims=True)
        o = jnp.dot(p.astype(BF16), vv[:, sl], preferred_element_type=F32)
        o_ref[:, sl] = (o / l).astype(BF16)


def _attn_bias(rel_bias):
    qpos = jnp.arange(AQ)[:, None]
    kpos = jnp.arange(AK)[None, :] - LEFT_CHUNKS * CHUNK
    idx = jnp.clip(qpos - kpos, -(CHUNK - 1), REL_MAX) + (CHUNK - 1)
    qc = qpos // CHUNK
    kc = jnp.arange(AK)[None, :] // CHUNK
    in_band = (kc >= qc) & (kc <= qc + LEFT_CHUNKS)
    bias = rel_bias.astype(F32)[:, idx]
    return jnp.where(in_band[None], bias, NEG_INF)


def _attn(q, k, v, bias):
    nq = SEQ // AQ
    qmap = lambda b, i: (b * nq + i, 0)
    kmap = lambda d: (lambda b, i: (b * nq + jnp.maximum(i - d, 0), 0))
    blk = lambda m: pl.BlockSpec((AQ, ATTN_WIDTH), m)
    return pl.pallas_call(
        _attn_kernel,
        grid=(BATCH, nq),
        in_specs=[blk(qmap), blk(kmap(2)), blk(kmap(1)), blk(kmap(0)),
                  blk(kmap(2)), blk(kmap(1)), blk(kmap(0)),
                  pl.BlockSpec((ATTN_HEADS, AQ, AK), lambda b, i: (0, 0, 0))],
        out_specs=blk(qmap),
        out_shape=jax.ShapeDtypeStruct((TOKENS, ATTN_WIDTH), BF16),
        compiler_params=pltpu.CompilerParams(
            dimension_semantics=("arbitrary", "arbitrary"), vmem_limit_bytes=VMEM_LIMIT),
        name="attn",
    )(q, k, k, k, v, v, v, bias)


def _ssm_tables(lam_re, lam_im, log_dt, b_re, b_im, c_re, c_im, d_skip):
    G, P, C, L = SSM_GROUPS, SSM_STATE, SSM_GROUP, SSM_L
    lam = lax.complex(lam_re.astype(F32), lam_im.astype(F32))
    dt = jnp.exp(log_dt.astype(F32))[:, None]
    lam_dt = lam * dt
    lam_bar = jnp.exp(lam_dt)
    b_bar = ((lam_bar - 1.0) / lam)[..., None] * lax.complex(b_re.astype(F32), b_im.astype(F32))
    c_mat = lax.complex(c_re.astype(F32), c_im.astype(F32))
    steps = jnp.arange(L + 1, dtype=F32)
    pw = jnp.exp(lam_dt[None] * steps[:, None, None])
    kern = jnp.einsum('gcp,tgp,gpd->tgcd', c_mat, pw[:L], b_bar).real
    s_idx = jnp.arange(L)[:, None]
    t_idx = jnp.arange(L)[None, :]
    lag = t_idx - s_idx
    kt = kern[jnp.clip(lag, 0, L - 1)]
    kt = jnp.where((lag >= 0)[:, :, None, None, None], kt, 0.0)
    tz = kt.transpose(2, 0, 4, 1, 3).reshape(G, L * C, L * C)
    w = pw[L - 1 - jnp.arange(L)][:, :, :, None] * b_bar[None]
    w = w.transpose(1, 0, 3, 2).reshape(G, L * C, P)
    vmat = c_mat.transpose(0, 2, 1)[:, :, None, :] * pw[1:].transpose(1, 2, 0)[:, :, :, None]
    vmat = vmat.reshape(G, P, L * C)
    a_chunk = pw[L]
    seg_steps = jnp.arange(SSM_SEG, dtype=F32) * L
    a_pow = jnp.exp(lam_dt[None] * seg_steps[:, None, None])
    a_seg = jnp.exp(lam_dt * float(SSM_SEG * L))

    def pair_lanes(x):
        return x.reshape(x.shape[:-2] + (SSM_PAIRS, 2 * P))

    zeros_w = jnp.zeros((SSM_PAIRS, L * C, P), F32)
    wr = w.real.reshape(SSM_PAIRS, 2, L * C, P)
    wi = w.imag.reshape(SSM_PAIRS, 2, L * C, P)
    we = jnp.concatenate([
        jnp.concatenate([wr[:, 0], zeros_w, wi[:, 0], zeros_w], axis=-1),
        jnp.concatenate([zeros_w, wr[:, 1], zeros_w, wi[:, 1]], axis=-1)], axis=1)
    zeros_v = jnp.zeros((SSM_PAIRS, P, L * C), F32)
    vr = vmat.real.reshape(SSM_PAIRS, 2, P, L * C)
    vi = -vmat.imag.reshape(SSM_PAIRS, 2, P, L * C)
    vs = jnp.concatenate([
        jnp.concatenate([vr[:, 0], zeros_v], axis=-1),
        jnp.concatenate([zeros_v, vr[:, 1]], axis=-1),
        jnp.concatenate([vi[:, 0], zeros_v], axis=-1),
        jnp.concatenate([zeros_v, vi[:, 1]], axis=-1)], axis=1)
    d_pair = jnp.broadcast_to(d_skip.astype(F32).reshape(SSM_PAIRS, 2, 1, C), (SSM_PAIRS, 2, L, C))
    d_pair = d_pair.reshape(SSM_PAIRS, 1, 2 * L * C)
    return dict(
        we=we.astype(BF16),
        tz=tz.reshape(SSM_PAIRS, 2, L * C, L * C).astype(BF16),
        vs=vs.astype(BF16),
        ar=pair_lanes(a_chunk.real)[:, None, :], ai=pair_lanes(a_chunk.imag)[:, None, :],
        pr=pair_lanes(a_pow.real).transpose(1, 0, 2), pi=pair_lanes(a_pow.imag).transpose(1, 0, 2),
        sr=pair_lanes(a_seg.real)[:, None, :], si=pair_lanes(a_seg.imag)[:, None, :],
        d=d_pair,
    )


def _ssm_kernel(u_ref, we_ref, tz_ref, vs_ref, ar_ref, ai_ref, pr_ref, pi_ref, sr_ref, si_ref,
                d_ref, y_ref, sre_ref, sim_ref):
    u = u_ref[0]
    e = jnp.dot(u, we_ref[0], preferred_element_type=F32)
    sre_ref[...] = e[:, 0:LANES]
    sim_ref[...] = e[:, LANES:2 * LANES]
    ar = ar_ref[0]
    ai = ai_ref[0]
    lre = jnp.zeros((SSM_NSEG, LANES), F32)
    lim = jnp.zeros((SSM_NSEG, LANES), F32)
    for r in range(SSM_SEG):
        rows = pl.ds(r, SSM_NSEG, stride=SSM_SEG)
        ere = sre_ref[rows, :]
        eim = sim_ref[rows, :]
        sre_ref[rows, :] = lre
        sim_ref[rows, :] = lim
        lre, lim = ar * lre - ai * lim + ere, ar * lim + ai * lre + eim
    sr = sr_ref[0]
    si = si_ref[0]
    segs_per_seq = SSM_NSEG // BATCH
    for b in range(BATCH):
        cre = jnp.zeros((1, LANES), F32)
        cim = jnp.zeros((1, LANES), F32)
        for s in range(segs_per_seq):
            seg = b * segs_per_seq + s
            if s > 0:
                rows = pl.ds(seg * SSM_SEG, SSM_SEG)
                pr = pr_ref[0]
                pi = pi_ref[0]
                sre_ref[rows, :] = sre_ref[rows, :] + pr * cre - pi * cim
                sim_ref[rows, :] = sim_ref[rows, :] + pr * cim + pi * cre
            lr = lre[seg:seg + 1, :]
            li = lim[seg:seg + 1, :]
            cre, cim = sr * cre - si * cim + lr, sr * cim + si * cre + li
    state = jnp.concatenate([sre_ref[...], sim_ref[...]], axis=1).astype(BF16)
    half = SSM_L * SSM_GROUP
    y = jnp.dot(state, vs_ref[0], preferred_element_type=F32)
    y0 = jnp.dot(u[:, 0:half], tz_ref[0, 0], preferred_element_type=F32)
    y1 = jnp.dot(u[:, half:2 * half], tz_ref[0, 1], preferred_element_type=F32)
    y = y + jnp.concatenate([y0, y1], axis=1) + d_ref[0] * u.astype(F32)
    y_ref[0] = y.astype(BF16)


def _ssm(u_pairs, tb):
    pair3 = lambda q: (q, 0, 0)
    pair4 = lambda q: (q, 0, 0, 0)
    half = SSM_L * SSM_GROUP
    vec = pl.BlockSpec((1, 1, LANES), pair3)
    tab = pl.BlockSpec((1, SSM_SEG, LANES), pair3)
    return pl.pallas_call(
        _ssm_kernel,
        grid=(SSM_PAIRS,),
        in_specs=[
            pl.BlockSpec((1, SSM_NCH, 2 * half), pair3),
            pl.BlockSpec((1, 2 * half, 2 * LANES), pair3),
            pl.BlockSpec((1, 2, half, half), pair4),
            pl.BlockSpec((1, 2 * LANES, 2 * half), pair3),
            vec, vec, tab, tab, vec, vec,
            pl.BlockSpec((1, 1, 2 * half), pair3),
        ],
        out_specs=pl.BlockSpec((1, SSM_NCH, 2 * half), pair3),
        out_shape=jax.ShapeDtypeStruct((SSM_PAIRS, SSM_NCH, 2 * half), BF16),
        scratch_shapes=[pltpu.VMEM((SSM_NCH, LANES), F32), pltpu.VMEM((SSM_NCH, LANES), F32)],
        compiler_params=pltpu.CompilerParams(
            dimension_semantics=("arbitrary",), vmem_limit_bytes=VMEM_LIMIT),
        name="ssm",
    )(u_pairs, tb['we'], tb['tz'], tb['vs'], tb['ar'], tb['ai'], tb['pr'], tb['pi'],
      tb['sr'], tb['si'], tb['d'])


def _to_pairs(u):
    x = u.reshape(SSM_NCH, SSM_L, SSM_PAIRS, 2, SSM_GROUP)
    return x.transpose(2, 0, 3, 1, 4).reshape(SSM_PAIRS, SSM_NCH, 2 * SSM_L * SSM_GROUP)


def _from_pairs(y):
    x = y.reshape(SSM_PAIRS, SSM_NCH, 2, SSM_L, SSM_GROUP)
    return x.transpose(1, 3, 0, 2, 4).reshape(TOKENS, SSM_WIDTH)


def _merge_kernel(x_ref, lg_ref, lb_ref, attn_ref, ys_ref, wg_ref, bg_ref, wua_ref, wglu_ref,
                  wus_ref, wo_ref, g1_ref, b1_ref, rw_ref, rb_ref, tri_ref,
                  hrow_ref, route_ref, gate_ref, cnt_ref, carry_ref):
    step = pl.program_id(0)

    @pl.when(step == 0)
    def _():
        carry_ref[...] = jnp.zeros_like(carry_ref)

    h0 = _layer_norm(x_ref[...], lg_ref[...], lb_ref[...])
    gates = _sigmoid(jnp.dot(h0.astype(BF16), wg_ref[...], preferred_element_type=F32) + bg_ref[...])
    y_attn = jnp.dot(attn_ref[...], wua_ref[...], preferred_element_type=F32)
    t = jax.nn.gelu(ys_ref[...].astype(F32)).astype(BF16)
    glu = jnp.dot(t, wglu_ref[...], preferred_element_type=F32)
    a = glu[:, 0:SSM_WIDTH] * _sigmoid(glu[:, SSM_WIDTH:2 * SSM_WIDTH])
    y_ssm = jnp.dot(a.astype(BF16), wus_ref[...], preferred_element_type=F32)
    mixin = gates[:, 0:D_MODEL] * y_attn + gates[:, D_MODEL:2 * D_MODEL] * y_ssm
    mix = jnp.dot(mixin.astype(BF16), wo_ref[...], preferred_element_type=F32)
    h1 = _layer_norm(DEEPNORM_ALPHA * h0 + mix, g1_ref[...], b1_ref[...])
    for j in range(ROW_TILES):
        hrow_ref[pl.ds(j, TM, stride=ROW_TILES), :] = h1[:, j * LANES:(j + 1) * LANES]

    logits = jnp.dot(h1, rw_ref[...], preferred_element_type=F32,
                     precision=lax.Precision.HIGHEST) + rb_ref[...]
    eio = lax.broadcasted_iota(I32, (TM, N_EXPERTS), 1)
    cur = logits
    vals, idxs = [], []
    for _ in range(TOP_K):
        m = jnp.max(cur, axis=-1, keepdims=True)
        ik = jnp.min(jnp.where(cur == m, eio, N_EXPERTS), axis=-1, keepdims=True)
        vals.append(m)
        idxs.append(ik)
        cur = jnp.where(eio == ik, -jnp.inf, cur)
    exps = [jnp.exp(v - vals[0]) for v in vals]
    denom = exps[0] + exps[1] + exps[2] + exps[3]
    onehot = jnp.zeros((TM, N_EXPERTS), F32)
    for ik in idxs:
        onehot = onehot + jnp.where(eio == ik, 1.0, 0.0)
    before = jnp.dot(tri_ref[...], onehot.astype(BF16), preferred_element_type=F32)
    before = before + carry_ref[0:1, 0:N_EXPERTS]
    lane = lax.broadcasted_iota(I32, (TM, LANES), 1)
    route = jnp.zeros((TM, LANES), I32)
    gate = jnp.zeros((TM, LANES), F32)
    for kk in range(TOP_K):
        rank = jnp.sum(jnp.where(eio == idxs[kk], before, 0.0), axis=-1, keepdims=True).astype(I32)
        route = jnp.where(lane == kk, idxs[kk], route)
        route = jnp.where(lane == TOP_K + kk, rank, route)
        gate = jnp.where(lane == kk, exps[kk] / denom, gate)
    route_ref[...] = route
    gate_ref[...] = gate
    carry_ref[0:1, 0:N_EXPERTS] = carry_ref[0:1, 0:N_EXPERTS] + jnp.sum(onehot, axis=0, keepdims=True)
    cnt_ref[...] = carry_ref[...]


def _merge(x2, ln_g, ln_b, attn_o, y_s, wg, bg, wua, wglu, wus, wo, g1, b1, rw, rb, tri):
    row = lambda i: (i, 0)
    const = lambda i: (0, 0)
    full = lambda a: pl.BlockSpec(a.shape, const)
    return pl.pallas_call(
        _merge_kernel,
        grid=(TOKENS // TM,),
        in_specs=[
            pl.BlockSpec((TM, D_MODEL), row), full(ln_g), full(ln_b),
            pl.BlockSpec((TM, ATTN_WIDTH), row), pl.BlockSpec((TM, SSM_WIDTH), row),
            full(wg), full(bg), full(wua), full(wglu), full(wus), full(wo), full(g1), full(b1),
            full(rw), full(rb), full(tri),
        ],
        out_specs=[
            pl.BlockSpec((TM * ROW_TILES, LANES), row),
            pl.BlockSpec((TM, LANES), row),
            pl.BlockSpec((TM, LANES), row),
            pl.BlockSpec((SUBLANES, LANES), const),
        ],
        out_shape=[
            jax.ShapeDtypeStruct((TOKENS * ROW_TILES, LANES), F32),
            jax.ShapeDtypeStruct((TOKENS, LANES), I32),
            jax.ShapeDtypeStruct((TOKENS, LANES), F32),
            jax.ShapeDtypeStruct((SUBLANES, LANES), F32),
        ],
        scratch_shapes=[pltpu.VMEM((SUBLANES, LANES), F32)],
        compiler_params=pltpu.CompilerParams(
            dimension_semantics=("arbitrary",), vmem_limit_bytes=VMEM_LIMIT),
        name="merge",
    )(x2, ln_g, ln_b, attn_o, y_s, wg, bg, wua, wglu, wus, wo, g1, b1, rw, rb, tri)


def _moe_kernel(bexp_ref, nvalid_ref, nact_ref, idx_hbm, hrow_hbm, wgu_ref, bgu_ref, wd_ref, bd_ref,
                yk_hbm, idx_smem, xbuf, obuf, wgu_bf, wd_bf, sems):
    b = pl.program_id(0)

    def row_slice(r):
        return pl.ds(pl.multiple_of(r * ROW_TILES, ROW_TILES), ROW_TILES)

    @pl.when(b == 0)
    def _():
        xbuf[...] = jnp.zeros_like(xbuf)

    @pl.when(b < nact_ref[0])
    def _():
        n_valid = nvalid_ref[b]
        moved = pl.ds(0, pl.multiple_of(n_valid * ROW_TILES, ROW_TILES))
        cp = pltpu.make_async_copy(idx_hbm.at[b], idx_smem, sems.at[0])
        cp.start()
        cp.wait()

        def gather(r, carry):
            tok = idx_smem[r >> 7, r & (LANES - 1)]
            pltpu.make_async_copy(hrow_hbm.at[row_slice(tok), :], xbuf.at[row_slice(r), :],
                                  sems.at[1]).start()
            return carry

        lax.fori_loop(0, n_valid, gather, 0)

        prev = bexp_ref[jnp.maximum(b - 1, 0)]

        @pl.when((b == 0) | (bexp_ref[b] != prev))
        def _():
            wgu_bf[...] = wgu_ref[0].astype(BF16)
            wd_bf[...] = wd_ref[0].astype(BF16)

        pltpu.make_async_copy(hrow_hbm.at[moved, :], xbuf.at[moved, :], sems.at[1]).wait()
        x = jnp.concatenate(
            [xbuf[pl.ds(j, EXPERT_BLOCK, stride=ROW_TILES), :] for j in range(ROW_TILES)],
            axis=1).astype(BF16)
        gu = jnp.dot(x, wgu_bf[...], preferred_element_type=F32) + bgu_ref[0]
        x_glu = jnp.minimum(gu[:, 0:D_EXPERT], SWIGLU_LIMIT)
        x_lin = jnp.clip(gu[:, D_EXPERT:2 * D_EXPERT], -SWIGLU_LIMIT, SWIGLU_LIMIT)
        act = x_glu * _sigmoid(SWIGLU_ALPHA * x_glu) * (x_lin + 1.0)
        out = jnp.dot(act.astype(BF16), wd_bf[...], preferred_element_type=F32) + bd_ref[0]
        for j in range(ROW_TILES):
            obuf[pl.ds(j, EXPERT_BLOCK, stride=ROW_TILES), :] = out[:, j * LANES:(j + 1) * LANES]

        def scatter(r, carry):
            slot = idx_smem[2 + (r >> 7), r & (LANES - 1)]
            pltpu.make_async_copy(obuf.at[row_slice(r), :], yk_hbm.at[row_slice(slot), :],
                                  sems.at[2]).start()
            return carry

        lax.fori_loop(0, n_valid, scatter, 0)
        pltpu.make_async_copy(obuf.at[moved, :], yk_hbm.at[moved, :], sems.at[2]).wait()


def _moe(blk_expert, blk_valid, n_active, idx_tbl, hrow, w_gate_up, b_gate_up, w_down, b_down):
    emap = lambda b, be, nv, na: (be[b], 0, 0)
    grid_spec = pltpu.PrefetchScalarGridSpec(
        num_scalar_prefetch=3,
        grid=(N_BLOCKS,),
        in_specs=[
            pl.BlockSpec(memory_space=pl.ANY),
            pl.BlockSpec(memory_space=pl.ANY),
            pl.BlockSpec((1, D_MODEL, 2 * D_EXPERT), emap),
            pl.BlockSpec((1, 1, 2 * D_EXPERT), emap),
            pl.BlockSpec((1, D_EXPERT, D_MODEL), emap),
            pl.BlockSpec((1, 1, D_MODEL), emap),
        ],
        out_specs=pl.BlockSpec(memory_space=pl.ANY),
        scratch_shapes=[
            pltpu.SMEM((SUBLANES, LANES), I32),
            pltpu.VMEM((EXPERT_BLOCK * ROW_TILES, LANES), F32),
            pltpu.VMEM((EXPERT_BLOCK * ROW_TILES, LANES), F32),
            pltpu.VMEM((D_MODEL, 2 * D_EXPERT), BF16),
            pltpu.VMEM((D_EXPERT, D_MODEL), BF16),
            pltpu.SemaphoreType.DMA((3,)),
        ],
    )
    return pl.pallas_call(
        _moe_kernel,
        grid_spec=grid_spec,
        out_shape=jax.ShapeDtypeStruct((N_ASSIGN * ROW_TILES, LANES), F32),
        compiler_params=pltpu.CompilerParams(
            dimension_semantics=("arbitrary",), vmem_limit_bytes=VMEM_LIMIT),
        name="moe",
    )(blk_expert, blk_valid, n_active, idx_tbl, hrow, w_gate_up, b_gate_up.reshape(N_EXPERTS, 1, -1),
      w_down, b_down.reshape(N_EXPERTS, 1, -1))


def _route_tables(route, counts):
    idx = route[:, 0:TOP_K]
    rank = route[:, TOP_K:2 * TOP_K]
    counts = counts.astype(I32)
    padded = ((counts + EXPERT_BLOCK - 1) // EXPERT_BLOCK) * EXPERT_BLOCK
    pends = jnp.cumsum(padded)
    pstarts = pends - padded
    dest = pstarts[idx] + rank
    tok = jnp.broadcast_to(jnp.arange(TOKENS, dtype=I32)[:, None], (TOKENS, TOP_K))
    slot = jnp.arange(TOP_K, dtype=I32)[None, :] * TOKENS + tok
    tok_pad = jnp.zeros((N_ROWS,), I32).at[dest.reshape(-1)].set(tok.reshape(-1))
    slot_pad = jnp.zeros((N_ROWS,), I32).at[dest.reshape(-1)].set(slot.reshape(-1))
    per_blk = EXPERT_BLOCK // LANES
    tbl = jnp.concatenate([
        tok_pad.reshape(N_BLOCKS, per_blk, LANES),
        slot_pad.reshape(N_BLOCKS, per_blk, LANES),
        jnp.zeros((N_BLOCKS, SUBLANES - 2 * per_blk, LANES), I32)], axis=1)
    blk_start = jnp.arange(N_BLOCKS, dtype=I32) * EXPERT_BLOCK
    blk_expert = jnp.minimum(jnp.searchsorted(pends, blk_start, side='right'), N_EXPERTS - 1).astype(I32)
    n_active = (pends[-1] // EXPERT_BLOCK).astype(I32).reshape(1)
    blk_valid = jnp.clip(counts[blk_expert] - (blk_start - pstarts[blk_expert]), 0, EXPERT_BLOCK).astype(I32)
    return blk_expert, blk_valid, n_active, tbl


def _final_kernel(hrow_ref, y0_ref, y1_ref, y2_ref, y3_ref, gate_ref, g2_ref, b2_ref, o_ref):
    gate = gate_ref[...]
    ys = (y0_ref, y1_ref, y2_ref, y3_ref)
    cols = []
    for j in range(ROW_TILES):
        rows = pl.ds(j, TM, stride=ROW_TILES)
        ffn = gate[:, 0:1] * ys[0][rows, :]
        for kk in range(1, TOP_K):
            ffn = ffn + gate[:, kk:kk + 1] * ys[kk][rows, :]
        cols.append(DEEPNORM_ALPHA * hrow_ref[rows, :] + ffn)
    z = jnp.concatenate(cols, axis=1)
    o_ref[...] = _layer_norm(z, g2_ref[...], b2_ref[...])


def _final(hrow, yk, gate, g2, b2):
    nt = TOKENS // TM
    row = lambda i: (i, 0)
    const = lambda i: (0, 0)
    kmap = lambda kk: (lambda i: (kk * nt + i, 0))
    rows_blk = lambda m: pl.BlockSpec((TM * ROW_TILES, LANES), m)
    return pl.pallas_call(
        _final_kernel,
        grid=(nt,),
        in_specs=[rows_blk(row), rows_blk(kmap(0)), rows_blk(kmap(1)), rows_blk(kmap(2)),
                  rows_blk(kmap(3)), pl.BlockSpec((TM, LANES), row),
                  pl.BlockSpec((1, D_MODEL), const), pl.BlockSpec((1, D_MODEL), const)],
        out_specs=pl.BlockSpec((TM, D_MODEL), row),
        out_shape=jax.ShapeDtypeStruct((TOKENS, D_MODEL), F32),
        compiler_params=pltpu.CompilerParams(
            dimension_semantics=("arbitrary",), vmem_limit_bytes=VMEM_LIMIT),
        name="final",
    )(hrow, yk, yk, yk, yk, gate, g2, b2)


def kernel(x, ln_in_g, ln_in_b, w_in, b_gate, rel_bias, lam_re, lam_im, log_dt, ssm_b_re, ssm_b_im,
           ssm_c_re, ssm_c_im, ssm_d, w_glu, w_up_attn, w_up_ssm, w_o, ln1_g, ln1_b, router_w,
           router_b, w_gate_up, b_gate_up, w_down, b_down, ln2_g, ln2_b):
    assert x.shape == (BATCH, SEQ, D_MODEL) and w_in.shape[0] == DEPTH
    l = 0
    x2 = x.reshape(TOKENS, D_MODEL)
    ln_g = ln_in_g.reshape(1, D_MODEL)
    ln_b = ln_in_b.reshape(1, D_MODEL)
    n_qkvu = 3 * ATTN_WIDTH + SSM_WIDTH
    w_qkvu = w_in[l, :, :n_qkvu].astype(BF16)
    w_gates = w_in[l, :, n_qkvu:].astype(BF16)

    q, k, v, u = _proj(x2, ln_g, ln_b, w_qkvu)
    attn_o = _attn(q, k, v, _attn_bias(rel_bias[l]))
    tables = _ssm_tables(lam_re[l], lam_im[l], log_dt[l], ssm_b_re[l], ssm_b_im[l],
                         ssm_c_re[l], ssm_c_im[l], ssm_d[l])
    y_s = _from_pairs(_ssm(_to_pairs(u), tables))

    tri = (jnp.arange(TM)[:, None] > jnp.arange(TM)[None, :]).astype(BF16)
    hrow, route, gate, counts = _merge(
        x2, ln_g, ln_b, attn_o, y_s, w_gates, b_gate[l].reshape(1, -1),
        w_up_attn[l].astype(BF16), w_glu[l].astype(BF16), w_up_ssm[l].astype(BF16),
        w_o[l].astype(BF16), ln1_g[l].reshape(1, -1), ln1_b[l].reshape(1, -1),
        router_w[l], router_b[l].reshape(1, -1), tri)

    blk_expert, blk_valid, n_active, idx_tbl = _route_tables(route, counts[0, :N_EXPERTS])
    yk = _moe(blk_expert, blk_valid, n_active, idx_tbl, hrow, w_gate_up[l], b_gate_up[l], w_down[l], b_down[l])
    out = _final(hrow, yk, gate, ln2_g[l].reshape(1, -1), ln2_b[l].reshape(1, -1))
    return out.reshape(BATCH, SEQ, D_MODEL)
```
